```python
import math
import jax, jax.numpy as jnp
from jax import lax
import numpy as np

D_MODEL = 2048
BATCH = 8
SEQ = 2048
DEPTH = 1

GRID_W = 64
Q_BLOCK = 128
A_HEADS = 8
A_HEAD_DIM = 64
A_V_DIM = 2 * A_HEAD_DIM
A_WIDTH = A_HEADS * A_V_DIM
B_HEADS = 8
B_KV_HEADS = 2
B_GROUP = B_HEADS // B_KV_HEADS
B_HEAD_DIM = 128
B_WIDTH = B_HEADS * B_HEAD_DIM
ROPE_SECTION = B_HEAD_DIM // 2
ROPE_THETA = 10000.0
N_BRANCHES = 2
IN_SIZES = (
    A_HEADS * 2 * A_HEAD_DIM,
    A_HEADS * 2 * A_HEAD_DIM,
    A_WIDTH,
    A_WIDTH,
    B_HEADS * B_HEAD_DIM,
    B_KV_HEADS * B_HEAD_DIM,
    B_KV_HEADS * B_HEAD_DIM,
    B_WIDTH,
    N_BRANCHES * D_MODEL,
)
IN_COLS = sum(IN_SIZES)
NORM_EPS = 1e-6

kernel_name = "hybrid_diffattn_gqa_axialrope_gated_merge"


def rms_norm(x, g):
    xf = x.astype(jnp.float32)
    y = xf * lax.rsqrt(jnp.mean(xf * xf, axis=-1, keepdims=True) + NORM_EPS)
    return (y * g.astype(jnp.float32)).astype(x.dtype)


def to_blocks(t, axis):
    shape = t.shape
    nb = shape[axis] // Q_BLOCK
    t = t.reshape(shape[:axis] + (nb, Q_BLOCK) + shape[axis + 1:])
    return jnp.moveaxis(t, axis, 0)


def from_blocks(t, axis):
    t = jnp.moveaxis(t, 0, axis)
    shape = t.shape
    return t.reshape(shape[:axis] + (shape[axis] * shape[axis + 1],) + shape[axis + 2:])


def diff_attention(q, k, v, lam, slopes):
    S = q.shape[3]
    pos = jnp.arange(S, dtype=jnp.float32)
    scale = A_HEAD_DIM ** -0.5

    def one_block(args):
        qi, pi = args
        s = jnp.einsum('bhmqd,bhmkd->bhmqk', qi, k).astype(jnp.float32) * scale
        s = s - slopes[:, None, None, None] * jnp.abs(pi[:, None] - pos[None, :])
        p = jax.nn.softmax(s, axis=-1)
        w = p[:, :, 0] - lam * p[:, :, 1]
        return jnp.einsum('bhqk,bhkd->bhqd', w.astype(v.dtype), v)

    o = lax.map(one_block, (to_blocks(q, 3), to_blocks(pos, 0)))
    return from_blocks(o, 2)


def gqa_attention(q, k, v):
    scale = B_HEAD_DIM ** -0.5

    def one_block(qi):
        s = jnp.einsum('bkgqd,bksd->bkgqs', qi, k).astype(jnp.float32) * scale
        p = jax.nn.softmax(s, axis=-1)
        return jnp.einsum('bkgqs,bksd->bkgqd', p.astype(v.dtype), v)

    o = lax.map(one_block, to_blocks(q, 3))
    return from_blocks(o, 3)


def rotate_section(xs, ang):
    x1, x2 = jnp.split(xs, 2, axis=-1)
    c = jnp.cos(ang).astype(xs.dtype)
    s = jnp.sin(ang).astype(xs.dtype)
    return jnp.concatenate([x1 * c - x2 * s, x2 * c + x1 * s], axis=-1)


def axial_rope(t, row, col):
    inv = ROPE_THETA ** (-jnp.arange(0, ROPE_SECTION, 2, dtype=jnp.float32) / ROPE_SECTION)
    ang_r = row[:, None] * inv[None, :]
    ang_c = col[:, None] * inv[None, :]
    return jnp.concatenate([rotate_section(t[..., :ROPE_SECTION], ang_r),
                            rotate_section(t[..., ROPE_SECTION:], ang_c)], axis=-1)


def setup_inputs(seed: int = 0) -> dict:
    key = jax.random.key(seed)
    ks = jax.random.split(key, 16)
    f32 = jnp.float32

    def gain(k, n):
        return (jnp.ones((DEPTH, n), f32) + 0.01 * jax.random.normal(k, (DEPTH, n), f32))

    return {
        "x": jax.random.normal(ks[0], (BATCH, SEQ, D_MODEL), f32),
        "norm_g": gain(ks[1], D_MODEL),
        "w_in": jax.random.normal(ks[2], (DEPTH, D_MODEL, IN_COLS), f32) * D_MODEL ** -0.5,
        "a_lambda_q1": 0.1 * jax.random.normal(ks[3], (DEPTH, A_HEAD_DIM), f32),
        "a_lambda_k1": 0.1 * jax.random.normal(ks[4], (DEPTH, A_HEAD_DIM), f32),
        "a_lambda_q2": 0.1 * jax.random.normal(ks[5], (DEPTH, A_HEAD_DIM), f32),
        "a_lambda_k2": 0.1 * jax.random.normal(ks[6], (DEPTH, A_HEAD_DIM), f32),
        "a_subln_g": gain(ks[7], A_V_DIM),
        "b_qnorm_g": gain(ks[8], B_HEAD_DIM),
        "b_knorm_g": gain(ks[9], B_HEAD_DIM),
        "w_proj_a": jax.random.normal(ks[10], (DEPTH, A_WIDTH, D_MODEL), f32) * A_WIDTH ** -0.5,
        "w_proj_b": jax.random.normal(ks[11], (DEPTH, B_WIDTH, D_MODEL), f32) * B_WIDTH ** -0.5,
        "w_out": jax.random.normal(ks[12], (DEPTH, D_MODEL, D_MODEL), f32) * D_MODEL ** -0.5,
        "final_g": jnp.ones((D_MODEL,), f32) + 0.01 * jax.random.normal(ks[13], (D_MODEL,), f32),
    }


def reference(x, norm_g, w_in, a_lambda_q1, a_lambda_k1, a_lambda_q2, a_lambda_k2,
              a_subln_g, b_qnorm_g, b_knorm_g, w_proj_a, w_proj_b, w_out, final_g):
    B, S, _ = x.shape
    rows = S // GRID_W
    row = jnp.repeat(jnp.arange(rows), GRID_W).astype(jnp.float32)
    col = jnp.tile(jnp.arange(GRID_W), rows).astype(jnp.float32)
    slopes = 2.0 ** (-8.0 * jnp.arange(1, A_HEADS + 1, dtype=jnp.float32) / A_HEADS)
    offsets = [int(o) for o in np.cumsum(IN_SIZES)[:-1]]

    h = x
    for l in range(DEPTH):
        u = rms_norm(h, norm_g[l])
        z = u @ w_in[l]
        aq, ak, av, ag, bq, bk, bv, bg, gm = jnp.split(z, offsets, axis=-1)

        aq = aq.reshape(B, S, A_HEADS, 2, A_HEAD_DIM).transpose(0, 2, 3, 1, 4)
        ak = ak.reshape(B, S, A_HEADS, 2, A_HEAD_DIM).transpose(0, 2, 3, 1, 4)
        av = av.reshape(B, S, A_HEADS, A_V_DIM).transpose(0, 2, 1, 3)
        lam_init = 0.8 - 0.6 * math.exp(-0.3 * l)
        lam = (jnp.exp(jnp.sum(a_lambda_q1[l].astype(jnp.float32) * a_lambda_k1[l].astype(jnp.float32)))
               - jnp.exp(jnp.sum(a_lambda_q2[l].astype(jnp.float32) * a_lambda_k2[l].astype(jnp.float32)))
               + lam_init)
        oa = diff_attention(aq, ak, av, lam, slopes)
        oa = rms_norm(oa, a_subln_g[l]) * (1.0 - lam_init)
        oa = oa.transpose(0, 2, 1, 3).reshape(B, S, A_WIDTH) * jax.nn.silu(ag)

        bq = rms_norm(bq.reshape(B, S, B_HEADS, B_HEAD_DIM), b_qnorm_g[l])
        bk = rms_norm(bk.reshape(B, S, B_KV_HEADS, B_HEAD_DIM), b_knorm_g[l])
        bq = bq.reshape(B, S, B_KV_HEADS, B_GROUP, B_HEAD_DIM).transpose(0, 2, 3, 1, 4)
        bk = bk.transpose(0, 2, 1, 3)
        bq = axial_rope(bq, row, col)
        bk = axial_rope(bk, row, col)
        bv = bv.reshape(B, S, B_KV_HEADS, B_HEAD_DIM).transpose(0, 2, 1, 3)
        ob = gqa_attention(bq, bk, bv)
        ob = ob.transpose(0, 3, 1, 2, 4).reshape(B, S, B_WIDTH) * jax.nn.silu(bg)

        ya = oa @ w_proj_a[l]
        yb = ob @ w_proj_b[l]
        ga, gb = jnp.split(jax.nn.sigmoid(gm), N_BRANCHES, axis=-1)
        h = h + (ga * ya + gb * yb) @ w_out[l]

    return rms_norm(h, final_g)
```

```python
import functools
import math

import jax
import jax.numpy as jnp
from jax import lax
from jax.experimental import pallas as pl
from jax.experimental.pallas import tpu as pltpu

GRID_W = 64
A_HEADS = 8
A_HEAD_DIM = 64
A_V_DIM = 2 * A_HEAD_DIM
A_WIDTH = A_HEADS * A_V_DIM
B_HEADS = 8
B_KV_HEADS = 2
B_GROUP = B_HEADS // B_KV_HEADS
B_HEAD_DIM = 128
B_WIDTH = B_HEADS * B_HEAD_DIM
ROPE_SECTION = B_HEAD_DIM // 2
ROPE_THETA = 10000.0
NORM_EPS = 1e-6

LANES = 128
VMEM_LIMIT_BYTES = 56 * 1024 * 1024

def _slab_layout(d_model):
    sizes = (
        ("gm", 2 * d_model),
        ("aq", A_WIDTH), ("ak", A_WIDTH), ("av", A_WIDTH), ("ag", A_WIDTH),
        ("bq", B_WIDTH), ("bk", B_KV_HEADS * B_HEAD_DIM), ("bv", B_KV_HEADS * B_HEAD_DIM),
        ("bg", B_WIDTH),
    )
    off, out = 0, {}
    for name, n in sizes:
        out[name] = (off, n)
        off += n
    return out, off


def _permute_w_in(w, d_model):
    n_rest = w.shape[1] - 2 * d_model
    return jnp.concatenate([w[:, n_rest:], w[:, :n_rest]], axis=1)


def _silu(t):
    return t * jax.nn.sigmoid(t)


def _norm_rope(t, gain, cos, sina, sinb, scale):
    ms = jnp.mean(t * t, axis=-1, keepdims=True)
    n = t * lax.rsqrt(ms + NORM_EPS) * gain
    r = n * cos + pltpu.roll(n, LANES - ROPE_SECTION // 2, 1) * sina + pltpu.roll(n, ROPE_SECTION // 2, 1) * sinb
    return r * scale


def _in_proj_kernel(x_ref, g_ref, w_ref, cos_ref, sina_ref, sinb_ref, qg_ref, kg_ref, z_ref, u_ref,
                    *, tn, layout, a_scale, b_scale, row_chunk):
    j = pl.program_id(1)
    tm = x_ref.shape[0]

    @pl.when(j == 0)
    def _():
        def body(r, c):
            rows = pl.ds(pl.multiple_of(r * row_chunk, row_chunk), row_chunk)
            xr = x_ref[rows, :]
            ms = jnp.mean(xr * xr, axis=-1, keepdims=True)
            u_ref[rows, :] = (xr * lax.rsqrt(ms + NORM_EPS) * g_ref[...]).astype(u_ref.dtype)
            return c
        lax.fori_loop(0, tm // row_chunk, body, 0)

    acc = jnp.dot(u_ref[...], w_ref[...], preferred_element_type=jnp.float32)

    def tiles(name):
        off, n = layout[name]
        return off // tn, (off + n) // tn

    def in_range(lo, hi):
        return jnp.logical_and(j >= lo, j < hi)

    gm_lo, gm_hi = tiles("gm")
    aq_lo, aq_hi = tiles("aq")
    ak_lo, _ = tiles("ak")
    _, av_hi = tiles("av")
    ag_lo, ag_hi = tiles("ag")
    bq_lo, bq_hi = tiles("bq")
    bk_lo, _ = tiles("bk")
    bg_lo, bg_hi = tiles("bg")
    n_k_cols = layout["bk"][1]

    @pl.when(in_range(gm_lo, gm_hi))
    def _():
        z_ref[...] = jax.nn.sigmoid(acc).astype(z_ref.dtype)

    @pl.when(in_range(aq_lo, aq_hi))
    def _():
        z_ref[...] = (acc * a_scale).astype(z_ref.dtype)

    @pl.when(in_range(ak_lo, av_hi))
    def _():
        z_ref[...] = acc.astype(z_ref.dtype)

    @pl.when(jnp.logical_or(in_range(ag_lo, ag_hi), in_range(bg_lo, bg_hi)))
    def _():
        z_ref[...] = _silu(acc).astype(z_ref.dtype)

    @pl.when(in_range(bq_lo, bq_hi))
    def _():
        for hh in range(tn // LANES):
            sl = slice(hh * LANES, (hh + 1) * LANES)
            z_ref[:, sl] = _norm_rope(acc[:, sl], qg_ref[...], cos_ref[...], sina_ref[...],
                                      sinb_ref[...], b_scale).astype(z_ref.dtype)

    @pl.when(j == bk_lo)
    def _():
        for hh in range(tn // LANES):
            sl = slice(hh * LANES, (hh + 1) * LANES)
            if hh * LANES < n_k_cols:
                z_ref[:, sl] = _norm_rope(acc[:, sl], kg_ref[...], cos_ref[...], sina_ref[...],
                                          sinb_ref[...], 1.0).astype(z_ref.dtype)
            else:
                z_ref[:, sl] = acc[:, sl].astype(z_ref.dtype)


def _in_proj(x2d, g, w_bf, cos_t, sina_t, sinb_t, qg, kg, *, seq, layout, n_cols, tm, tn):
    T, D = x2d.shape
    assert T % tm == 0 and seq % tm == 0 and n_cols % tn == 0
    for name, (off, n) in layout.items():
        if name in ("bk", "bv"):
            continue
        assert off % tn == 0 and n % tn == 0, name
    assert layout["bk"][0] % tn == 0 and layout["bk"][1] + layout["bv"][1] == tn
    seq_tiles = seq // tm
    kern = functools.partial(
        _in_proj_kernel, tn=tn, layout=layout, a_scale=A_HEAD_DIM ** -0.5, b_scale=B_HEAD_DIM ** -0.5,
        row_chunk=min(tm, 256))
    tab_spec = pl.BlockSpec((tm, LANES), lambda i, j: (i % seq_tiles, 0))
    vec_spec = pl.BlockSpec((1, LANES), lambda i, j: (0, 0))
    return pl.pallas_call(
        kern,
        out_shape=jax.ShapeDtypeStruct((T, n_cols), jnp.bfloat16),
        grid=(T // tm, n_cols // tn),
        in_specs=[
            pl.BlockSpec((tm, D), lambda i, j: (i, 0)),
            pl.BlockSpec((1, D), lambda i, j: (0, 0)),
            pl.BlockSpec((D, tn), lambda i, j: (0, j)),
            tab_spec, tab_spec, tab_spec, vec_spec, vec_spec,
        ],
        out_specs=pl.BlockSpec((tm, tn), lambda i, j: (i, j)),
        scratch_shapes=[pltpu.VMEM((tm, D), jnp.bfloat16)],
        compiler_params=pltpu.CompilerParams(
            dimension_semantics=("parallel", "arbitrary"), vmem_limit_bytes=VMEM_LIMIT_BYTES),
        name="in_proj",
    )(x2d, g, w_bf, cos_t, sina_t, sinb_t, qg, kg)


def _softmax_parts(s):
    m = jnp.max(s, axis=-1, keepdims=True)
    e = jnp.exp(s - m)
    return e, jnp.sum(e, axis=-1, keepdims=True)


def _attn_a_kernel(slopes_ref, q_ref, k_ref, v_ref, gate_ref, lq1_ref, lk1_ref, lq2_ref, lk2_ref, sg_ref,
                   o_ref, *, lam_init):
    h = pl.program_id(1)
    qi = pl.program_id(2)
    tq = q_ref.shape[0]
    S = k_ref.shape[0]

    lam = (jnp.exp(jnp.sum(lq1_ref[...] * lk1_ref[...], axis=-1, keepdims=True))
           - jnp.exp(jnp.sum(lq2_ref[...] * lk2_ref[...], axis=-1, keepdims=True)) + lam_init)

    q = q_ref[...]
    k = k_ref[...]
    lane = lax.broadcasted_iota(jnp.int32, (1, LANES), 1)
    zero = jnp.zeros_like(q)
    q1 = jnp.where(lane < A_HEAD_DIM, q, zero)
    q2 = jnp.where(lane >= A_HEAD_DIM, q, zero)
    dn = (((1,), (1,)), ((), ()))
    s1 = lax.dot_general(q1, k, dn, preferred_element_type=jnp.float32)
    s2 = lax.dot_general(q2, k, dn, preferred_element_type=jnp.float32)

    rows = (qi * tq + lax.broadcasted_iota(jnp.int32, (tq, 1), 0)).astype(jnp.float32)
    cols = lax.broadcasted_iota(jnp.int32, (1, S), 1).astype(jnp.float32)
    bias = jnp.abs(rows - cols) * (-slopes_ref[h])
    e1, l1 = _softmax_parts(s1 + bias)
    e2, l2 = _softmax_parts(s2 + bias)
    w = e1 * (1.0 / l1) - e2 * (lam / l2)
    o = jnp.dot(w.astype(v_ref.dtype), v_ref[...], preferred_element_type=jnp.float32)

    ms = jnp.mean(o * o, axis=-1, keepdims=True)
    o = o * lax.rsqrt(ms + NORM_EPS) * sg_ref[...] * (1.0 - lam_init)
    o_ref[...] = (o * gate_ref[...].astype(jnp.float32)).astype(o_ref.dtype)


def _attn_a(z, slopes, lq1, lk1, lq2, lk2, sg, *, batch, seq, layout, tq, lam_init):
    T = z.shape[0]
    nq = seq // tq
    cb = lambda name: layout[name][0] // LANES
    aq, ak, av, ag = cb("aq"), cb("ak"), cb("av"), cb("ag")
    lam_spec = pl.BlockSpec((1, A_HEAD_DIM), lambda b, h, i: (0, 0))
    return pl.pallas_call(
        functools.partial(_attn_a_kernel, lam_init=lam_init),
        out_shape=jax.ShapeDtypeStruct((T, A_WIDTH), jnp.bfloat16),
        grid=(batch, A_HEADS, nq),
        in_specs=[
            pl.BlockSpec(memory_space=pltpu.SMEM),
            pl.BlockSpec((tq, LANES), lambda b, h, i: (b * nq + i, aq + h)),
            pl.BlockSpec((seq, LANES), lambda b, h, i: (b, ak + h)),
            pl.BlockSpec((seq, LANES), lambda b, h, i: (b, av + h)),
            pl.BlockSpec((tq, LANES), lambda b, h, i: (b * nq + i, ag + h)),
            lam_spec, lam_spec, lam_spec, lam_spec,
            pl.BlockSpec((1, LANES), lambda b, h, i: (0, 0)),
        ],
        out_specs=pl.BlockSpec((tq, LANES), lambda b, h, i: (b * nq + i, h)),
        compiler_params=pltpu.CompilerParams(
            dimension_semantics=("parallel", "parallel", "arbitrary"), vmem_limit_bytes=VMEM_LIMIT_BYTES),
        name="attn_a",
    )(slopes, z, z, z, z, lq1, lk1, lq2, lk2, sg)


def _attn_b_kernel(q_ref, k_ref, v_ref, gate_ref, o_ref):
    k = k_ref[...]
    v = v_ref[...]
    dn = (((1,), (1,)), ((), ()))
    for g in range(B_GROUP):
        sl = slice(g * B_HEAD_DIM, (g + 1) * B_HEAD_DIM)
        s = lax.dot_general(q_ref[:, sl], k, dn, preferred_element_type=jnp.float32)
        e, l = _softmax_parts(s)
        o = jnp.dot(e.astype(v.dtype), v, preferred_element_type=jnp.float32) * (1.0 / l)
        o_ref[:, sl] = (o * gate_ref[:, sl].astype(jnp.float32)).astype(o_ref.dtype)


def _attn_b(z, *, batch, seq, layout, tq):
    T = z.shape[0]
    nq = seq // tq
    gw = B_GROUP * B_HEAD_DIM
    bq = layout["bq"][0] // gw
    bg = layout["bg"][0] // gw
    bk = layout["bk"][0] // B_HEAD_DIM
    bv = layout["bv"][0] // B_HEAD_DIM
    assert layout["bq"][0] % gw == 0 and layout["bg"][0] % gw == 0
    return pl.pallas_call(
        _attn_b_kernel,
        out_shape=jax.ShapeDtypeStruct((T, B_WIDTH), jnp.bfloat16),
        grid=(batch, B_KV_HEADS, nq),
        in_specs=[
            pl.BlockSpec((tq, gw), lambda b, c, i: (b * nq + i, bq + c)),
            pl.BlockSpec((seq, B_HEAD_DIM), lambda b, c, i: (b, bk + c)),
            pl.BlockSpec((seq, B_HEAD_DIM), lambda b, c, i: (b, bv + c)),
            pl.BlockSpec((tq, gw), lambda b, c, i: (b * nq + i, bg + c)),
        ],
        out_specs=pl.BlockSpec((tq, gw), lambda b, c, i: (b * nq + i, c)),
        compiler_params=pltpu.CompilerParams(
            dimension_semantics=("parallel", "parallel", "arbitrary"), vmem_limit_bytes=VMEM_LIMIT_BYTES),
        name="attn_b",
    )(z, z, z, z)


def _out_proj_kernel(x_ref, oa_ref, ob_ref, ga_ref, gb_ref, wa_ref, wb_ref, wo_ref, fg_ref, o_ref, *, final_norm):
    ya = jnp.dot(oa_ref[...], wa_ref[...], preferred_element_type=jnp.float32)
    yb = jnp.dot(ob_ref[...], wb_ref[...], preferred_element_type=jnp.float32)
    m = ga_ref[...].astype(jnp.float32) * ya + gb_ref[...].astype(jnp.float32) * yb
    hres = x_ref[...] + jnp.dot(m.astype(wo_ref.dtype), wo_ref[...], preferred_element_type=jnp.float32)
    if final_norm:
        ms = jnp.mean(hres * hres, axis=-1, keepdims=True)
        hres = hres * lax.rsqrt(ms + NORM_EPS) * fg_ref[...]
    o_ref[...] = hres.astype(o_ref.dtype)


def _out_proj(x2d, oa, ob, z, wa, wb, wo, fg, *, tm, final_norm):
    T, D = x2d.shape
    const = lambda shape: pl.BlockSpec(shape, lambda i: (0, 0), pipeline_mode=pl.Buffered(1))
    return pl.pallas_call(
        functools.partial(_out_proj_kernel, final_norm=final_norm),
        out_shape=jax.ShapeDtypeStruct((T, D), x2d.dtype),
        grid=(T // tm,),
        in_specs=[
            pl.BlockSpec((tm, D), lambda i: (i, 0)),
            pl.BlockSpec((tm, A_WIDTH), lambda i: (i, 0)),
            pl.BlockSpec((tm, B_WIDTH), lambda i: (i, 0)),
            pl.BlockSpec((tm, D), lambda i: (i, 0)),
            pl.BlockSpec((tm, D), lambda i: (i, 1)),
            const((A_WIDTH, D)), const((B_WIDTH, D)), const((D, D)), const((1, D)),
        ],
        out_specs=pl.BlockSpec((tm, D), lambda i: (i, 0)),
        compiler_params=pltpu.CompilerParams(
            dimension_semantics=("parallel",), vmem_limit_bytes=VMEM_LIMIT_BYTES),
        name="out_proj",
    )(x2d, oa, ob, z, z, wa, wb, wo, fg)


def _rope_tables(seq):
    rows = seq // GRID_W
    row = jnp.repeat(jnp.arange(rows), GRID_W).astype(jnp.float32)
    col = jnp.tile(jnp.arange(GRID_W), rows).astype(jnp.float32)
    inv = ROPE_THETA ** (-jnp.arange(0, ROPE_SECTION, 2, dtype=jnp.float32) / ROPE_SECTION)
    half = ROPE_SECTION // 2
    zeros = jnp.zeros((seq, half), jnp.float32)
    cos_parts, sina_parts, sinb_parts = [], [], []
    for pos in (row, col):
        ang = pos[:, None] * inv[None, :]
        c, s = jnp.cos(ang), jnp.sin(ang)
        cos_parts += [c, c]
        sina_parts += [-s, zeros]
        sinb_parts += [zeros, s]
    cat = lambda parts: jnp.concatenate(parts, axis=-1)
    return cat(cos_parts), cat(sina_parts), cat(sinb_parts)


def kernel(x, norm_g, w_in, a_lambda_q1, a_lambda_k1, a_lambda_q2, a_lambda_k2, a_subln_g, b_qnorm_g,
           b_knorm_g, w_proj_a, w_proj_b, w_out, final_g):
    B, S, D = x.shape
    depth = norm_g.shape[0]
    T = B * S
    layout, n_cols = _slab_layout(D)
    assert n_cols == w_in.shape[-1]
    bf16 = jnp.bfloat16

    tm_in = min(1024, S)
    tn_in = 512
    tq_a = min(256, S)
    tq_b = min(256, S)
    tm_out = min(512, T)

    cos_t, sina_t, sinb_t = _rope_tables(S)
    slopes = 2.0 ** (-8.0 * jnp.arange(1, A_HEADS + 1, dtype=jnp.float32) / A_HEADS)

    h = x.reshape(T, D)
    for l in range(depth):
        lam_init = 0.8 - 0.6 * math.exp(-0.3 * l)
        w_bf = _permute_w_in(w_in[l], D).astype(bf16)
        z = _in_proj(h, norm_g[l][None, :], w_bf, cos_t, sina_t, sinb_t, b_qnorm_g[l][None, :],
                     b_knorm_g[l][None, :], seq=S, layout=layout, n_cols=n_cols, tm=tm_in, tn=tn_in)
        oa = _attn_a(z, slopes, a_lambda_q1[l][None, :], a_lambda_k1[l][None, :], a_lambda_q2[l][None, :],
                     a_lambda_k2[l][None, :], a_subln_g[l][None, :], batch=B, seq=S, layout=layout,
                     tq=tq_a, lam_init=lam_init)
        ob = _attn_b(z, batch=B, seq=S, layout=layout, tq=tq_b)
        h = _out_proj(h, oa, ob, z, w_proj_a[l].astype(bf16), w_proj_b[l].astype(bf16),
                      w_out[l].astype(bf16), final_g[None, :], tm=tm_out, final_norm=(l == depth - 1))
    return h.reshape(B, S, D)
```

```python
import functools
import math

import jax
import jax.numpy as jnp
from jax import lax
from jax.experimental import pallas as pl
from jax.experimental.pallas import tpu as pltpu

GRID_W = 64
A_HEADS = 8
A_HEAD_DIM = 64
A_V_DIM = 2 * A_HEAD_DIM
A_WIDTH = A_HEADS * A_V_DIM
B_HEADS = 8
B_KV_HEADS = 2
B_GROUP = B_HEADS // B_KV_HEADS
B_HEAD_DIM = 128
B_WIDTH = B_HEADS * B_HEAD_DIM
ROPE_SECTION = B_HEAD_DIM // 2
ROPE_THETA = 10000.0
NORM_EPS = 1e-6
LOG2E = math.log2(math.e)

LANES = 128
VMEM_LIMIT_BYTES = 56 * 1024 * 1024


def _slab_layout(d_model):
    sizes = (
        ("aq", A_WIDTH), ("ak", A_WIDTH), ("av", A_WIDTH), ("ag", A_WIDTH),
        ("bq", B_WIDTH), ("bk", B_KV_HEADS * B_HEAD_DIM), ("bv", B_KV_HEADS * B_HEAD_DIM),
        ("bg", B_WIDTH),
        ("gm", 2 * d_model),
    )
    off, out = 0, {}
    for name, n in sizes:
        out[name] = (off, n)
        off += n
    return out, off


def _silu(t):
    return t * jax.nn.sigmoid(t)


def _norm_rope(t, gain, cos, sina, sinb, scale):
    ms = jnp.mean(t * t, axis=-1, keepdims=True)
    n = t * lax.rsqrt(ms + NORM_EPS) * gain
    r = n * cos + pltpu.roll(n, LANES - ROPE_SECTION // 2, 1) * sina + pltpu.roll(n, ROPE_SECTION // 2, 1) * sinb
    return r * scale


def _in_proj_kernel(x_ref, g_ref, w_ref, cos_ref, sina_ref, sinb_ref, qg_ref, kg_ref, z_ref, u_ref,
                    *, tn, layout, a_scale, b_scale, row_chunk):
    j = pl.program_id(1)
    tm = x_ref.shape[0]
    n_chunks = tm // row_chunk

    @pl.when(j == 0)
    def _():
        def body(r, c):
            rows = pl.ds(pl.multiple_of(r * row_chunk, row_chunk), row_chunk)
            xr = x_ref[rows, :]
            ms = jnp.mean(xr * xr, axis=-1, keepdims=True)
            u_ref[rows, :] = (xr * lax.rsqrt(ms + NORM_EPS) * g_ref[...]).astype(u_ref.dtype)
            return c
        lax.fori_loop(0, n_chunks, body, 0)

    def project(epilogue):
        for r in range(n_chunks):
            rows = slice(r * row_chunk, (r + 1) * row_chunk)
            acc = jnp.dot(u_ref[rows, :], w_ref[...], preferred_element_type=jnp.float32)
            z_ref[rows, :] = epilogue(acc, rows).astype(z_ref.dtype)

    def heads(fn_of_head):
        def epilogue(acc, rows):
            cols = [fn_of_head(hh, acc[:, hh * LANES:(hh + 1) * LANES], rows) for hh in range(tn // LANES)]
            return jnp.concatenate(cols, axis=-1)
        return epilogue

    def rope(t, gain_ref, rows, scale):
        return _norm_rope(t, gain_ref[...], cos_ref[rows, :], sina_ref[rows, :], sinb_ref[rows, :], scale)

    def tiles(name):
        off, n = layout[name]
        return off // tn, (off + n) // tn

    def in_range(lo, hi):
        return jnp.logical_and(j >= lo, j < hi)

    gm_lo, gm_hi = tiles("gm")
    aq_lo, aq_hi = tiles("aq")
    ak_lo, _ = tiles("ak")
    _, av_hi = tiles("av")
    ag_lo, ag_hi = tiles("ag")
    bq_lo, bq_hi = tiles("bq")
    bk_lo, _ = tiles("bk")
    bg_lo, bg_hi = tiles("bg")
    n_k_cols = layout["bk"][1]

    @pl.when(in_range(gm_lo, gm_hi))
    def _():
        project(lambda acc, rows: jax.nn.sigmoid(acc))

    @pl.when(in_range(aq_lo, aq_hi))
    def _():
        project(lambda acc, rows: acc * a_scale)

    @pl.when(in_range(ak_lo, av_hi))
    def _():
        project(lambda acc, rows: acc)

    @pl.when(jnp.logical_or(in_range(ag_lo, ag_hi), in_range(bg_lo, bg_hi)))
    def _():
        project(lambda acc, rows: _silu(acc))

    @pl.when(in_range(bq_lo, bq_hi))
    def _():
        project(heads(lambda hh, t, rows: rope(t, qg_ref, rows, b_scale)))

    @pl.when(j == bk_lo)
    def _():
        project(heads(lambda hh, t, rows: rope(t, kg_ref, rows, 1.0) if hh * LANES < n_k_cols else t))


def _in_proj(x2d, g, w_bf, cos_t, sina_t, sinb_t, qg, kg, *, seq, layout, n_cols, tm, tn):
    T, D = x2d.shape
    assert T % tm == 0 and seq % tm == 0 and n_cols % tn == 0
    for name, (off, n) in layout.items():
        if name in ("bk", "bv"):
            continue
        assert off % tn == 0 and n % tn == 0, name
    assert layout["bk"][0] % tn == 0 and layout["bk"][1] + layout["bv"][1] == tn
    seq_tiles = seq // tm
    kern = functools.partial(
        _in_proj_kernel, tn=tn, layout=layout, a_scale=A_HEAD_DIM ** -0.5 * LOG2E,
        b_scale=B_HEAD_DIM ** -0.5 * LOG2E, row_chunk=min(tm, 256))
    tab_spec = pl.BlockSpec((tm, LANES), lambda i, j: (i % seq_tiles, 0))
    vec_spec = pl.BlockSpec((1, LANES), lambda i, j: (0, 0))
    return pl.pallas_call(
        kern,
        out_shape=jax.ShapeDtypeStruct((T, n_cols), jnp.bfloat16),
        grid=(T // tm, n_cols // tn),
        in_specs=[
            pl.BlockSpec((tm, D), lambda i, j: (i, 0)),
            pl.BlockSpec((1, D), lambda i, j: (0, 0)),
            pl.BlockSpec((D, tn), lambda i, j: (0, j)),
            tab_spec, tab_spec, tab_spec, vec_spec, vec_spec,
        ],
        out_specs=pl.BlockSpec((tm, tn), lambda i, j: (i, j)),
        scratch_shapes=[pltpu.VMEM((tm, D), jnp.bfloat16)],
        compiler_params=pltpu.CompilerParams(
            dimension_semantics=("parallel", "arbitrary"), vmem_limit_bytes=VMEM_LIMIT_BYTES),
        name="in_proj",
    )(x2d, g, w_bf, cos_t, sina_t, sinb_t, qg, kg)


_NT = (((1,), (1,)), ((), ()))
ONES_ROWS = 16


def _fill_vxt(vxt_ref, v_ref):
    d = v_ref.shape[1]
    vxt_ref[:d, :] = v_ref[...].astype(jnp.float32).T.astype(vxt_ref.dtype)
    vxt_ref[d:, :] = jnp.ones((vxt_ref.shape[0] - d, vxt_ref.shape[1]), vxt_ref.dtype)


def _softmax_pv(t, vxt):
    d = vxt.shape[0] - ONES_ROWS
    e = jnp.exp2(t - jnp.max(t, axis=0, keepdims=True)).astype(vxt.dtype)
    ox = jnp.dot(vxt, e, preferred_element_type=jnp.float32)
    return ox[:d, :], ox[d:d + 1, :]


def _attn_a_kernel(slopes_ref, q_ref, k_ref, v_ref, gate_ref, lq1_ref, lk1_ref, lq2_ref, lk2_ref, sg_ref,
                   o_ref, bias_ref, vxt_ref, *, lam_init):
    h = pl.program_id(1)
    qi = pl.program_id(2)
    tq = q_ref.shape[0]
    S = k_ref.shape[0]
    pad = S - tq

    @pl.when(qi == 0)
    def _():
        rows = lax.broadcasted_iota(jnp.int32, (bias_ref.shape[0], 1), 0).astype(jnp.float32)
        cols = lax.broadcasted_iota(jnp.int32, (1, tq), 1).astype(jnp.float32)
        bias_ref[...] = jnp.abs(rows - cols - float(pad)) * (-slopes_ref[h])
        _fill_vxt(vxt_ref, v_ref)

    bias = bias_ref[pl.ds(pl.multiple_of(pad - qi * tq, tq), S), :]

    lam = (jnp.exp(jnp.sum(lq1_ref[...] * lk1_ref[...], axis=-1, keepdims=True))
           - jnp.exp(jnp.sum(lq2_ref[...] * lk2_ref[...], axis=-1, keepdims=True)) + lam_init)

    q = q_ref[...]
    k = k_ref[...]
    vxt = vxt_ref[...]
    lane = lax.broadcasted_iota(jnp.int32, (1, LANES), 1)
    zero = jnp.zeros_like(q)
    q1 = jnp.where(lane < A_HEAD_DIM, q, zero)
    q2 = jnp.where(lane >= A_HEAD_DIM, q, zero)
    t = lax.dot_general(k, jnp.concatenate([q1, q2], axis=0), _NT, preferred_element_type=jnp.float32)
    o1, l1 = _softmax_pv(t[:, :tq] + bias, vxt)
    o2, l2 = _softmax_pv(t[:, tq:] + bias, vxt)
    o = (o1 * (1.0 / l1) - o2 * (lam / l2)).T

    ms = jnp.mean(o * o, axis=-1, keepdims=True)
    o = o * lax.rsqrt(ms + NORM_EPS) * sg_ref[...] * (1.0 - lam_init)
    o_ref[...] = (o * gate_ref[...].astype(jnp.float32)).astype(o_ref.dtype)


def _attn_a(z, slopes, lq1, lk1, lq2, lk2, sg, *, batch, seq, layout, tq, lam_init):
    T = z.shape[0]
    nq = seq // tq
    cb = lambda name: layout[name][0] // LANES
    aq, ak, av, ag = cb("aq"), cb("ak"), cb("av"), cb("ag")
    lam_spec = pl.BlockSpec((1, A_HEAD_DIM), lambda b, h, i: (0, 0))
    return pl.pallas_call(
        functools.partial(_attn_a_kernel, lam_init=lam_init),
        out_shape=jax.ShapeDtypeStruct((T, A_WIDTH), jnp.bfloat16),
        grid=(batch, A_HEADS, nq),
        in_specs=[
            pl.BlockSpec(memory_space=pltpu.SMEM),
            pl.BlockSpec((tq, LANES), lambda b, h, i: (b * nq + i, aq + h)),
            pl.BlockSpec((seq, LANES), lambda b, h, i: (b, ak + h)),
            pl.BlockSpec((seq, LANES), lambda b, h, i: (b, av + h)),
            pl.BlockSpec((tq, LANES), lambda b, h, i: (b * nq + i, ag + h)),
            lam_spec, lam_spec, lam_spec, lam_spec,
            pl.BlockSpec((1, LANES), lambda b, h, i: (0, 0)),
        ],
        out_specs=pl.BlockSpec((tq, LANES), lambda b, h, i: (b * nq + i, h)),
        scratch_shapes=[pltpu.VMEM((2 * seq - tq, tq), jnp.float32),
                        pltpu.VMEM((A_V_DIM + ONES_ROWS, seq), jnp.bfloat16)],
        compiler_params=pltpu.CompilerParams(
            dimension_semantics=("parallel", "parallel", "arbitrary"), vmem_limit_bytes=VMEM_LIMIT_BYTES),
        name="attn_a",
    )(slopes, z, z, z, z, lq1, lk1, lq2, lk2, sg)


def _attn_b_kernel(q_ref, k_ref, v_ref, gate_ref, o_ref, vxt_ref):
    @pl.when(pl.program_id(2) == 0)
    def _():
        _fill_vxt(vxt_ref, v_ref)

    k = k_ref[...]
    vxt = vxt_ref[...]

    def scores(g):
        return lax.dot_general(k, q_ref[:, g * B_HEAD_DIM:(g + 1) * B_HEAD_DIM], _NT,
                               preferred_element_type=jnp.float32)

    t_next = scores(0)
    for g in range(B_GROUP):
        sl = slice(g * B_HEAD_DIM, (g + 1) * B_HEAD_DIM)
        t = t_next
        if g + 1 < B_GROUP:
            t_next = scores(g + 1)
        ot, l = _softmax_pv(t, vxt)
        o = (ot * (1.0 / l)).T
        o_ref[:, sl] = (o * gate_ref[:, sl].astype(jnp.float32)).astype(o_ref.dtype)


def _attn_b(z, *, batch, seq, layout, tq):
    T = z.shape[0]
    nq = seq // tq
    gw = B_GROUP * B_HEAD_DIM
    bq = layout["bq"][0] // gw
    bg = layout["bg"][0] // gw
    bk = layout["bk"][0] // B_HEAD_DIM
    bv = layout["bv"][0] // B_HEAD_DIM
    assert layout["bq"][0] % gw == 0 and layout["bg"][0] % gw == 0
    return pl.pallas_call(
        _attn_b_kernel,
        out_shape=jax.ShapeDtypeStruct((T, B_WIDTH), jnp.bfloat16),
        grid=(batch, B_KV_HEADS, nq),
        in_specs=[
            pl.BlockSpec((tq, gw), lambda b, c, i: (b * nq + i, bq + c)),
            pl.BlockSpec((seq, B_HEAD_DIM), lambda b, c, i: (b, bk + c)),
            pl.BlockSpec((seq, B_HEAD_DIM), lambda b, c, i: (b, bv + c)),
            pl.BlockSpec((tq, gw), lambda b, c, i: (b * nq + i, bg + c)),
        ],
        out_specs=pl.BlockSpec((tq, gw), lambda b, c, i: (b * nq + i, c)),
        scratch_shapes=[pltpu.VMEM((B_HEAD_DIM + ONES_ROWS, seq), jnp.bfloat16)],
        compiler_params=pltpu.CompilerParams(
            dimension_semantics=("parallel", "parallel", "arbitrary"), vmem_limit_bytes=VMEM_LIMIT_BYTES),
        name="attn_b",
    )(z, z, z, z)


def _out_proj_kernel(*refs, n_gate_blocks, final_norm):
    x_ref, oa_ref, ob_ref = refs[:3]
    ga_refs = refs[3:3 + n_gate_blocks]
    gb_refs = refs[3 + n_gate_blocks:3 + 2 * n_gate_blocks]
    wa_ref, wb_ref, wo_ref, fg_ref, o_ref = refs[3 + 2 * n_gate_blocks:]
    ya = jnp.dot(oa_ref[...], wa_ref[...], preferred_element_type=jnp.float32)
    yb = jnp.dot(ob_ref[...], wb_ref[...], preferred_element_type=jnp.float32)
    ga = jnp.concatenate([r[...] for r in ga_refs], axis=-1).astype(jnp.float32)
    gb = jnp.concatenate([r[...] for r in gb_refs], axis=-1).astype(jnp.float32)
    m = ga * ya + gb * yb
    hres = x_ref[...] + jnp.dot(m.astype(wo_ref.dtype), wo_ref[...], preferred_element_type=jnp.float32)
    if final_norm:
        ms = jnp.mean(hres * hres, axis=-1, keepdims=True)
        hres = hres * lax.rsqrt(ms + NORM_EPS) * fg_ref[...]
    o_ref[...] = hres.astype(o_ref.dtype)


def _out_proj(x2d, oa, ob, z, wa, wb, wo, fg, *, layout, gate_block, tm, final_norm):
    T, D = x2d.shape
    gm_off, gm_n = layout["gm"]
    assert gm_n == 2 * D and gm_off % gate_block == 0 and D % gate_block == 0
    n_gate_blocks = D // gate_block
    const = lambda shape: pl.BlockSpec(shape, lambda i: (0, 0), pipeline_mode=pl.Buffered(1))
    gate_spec = lambda c: pl.BlockSpec((tm, gate_block), lambda i: (i, c))
    first = gm_off // gate_block
    gate_specs = [gate_spec(first + c) for c in range(2 * n_gate_blocks)]
    return pl.pallas_call(
        functools.partial(_out_proj_kernel, n_gate_blocks=n_gate_blocks, final_norm=final_norm),
        out_shape=jax.ShapeDtypeStruct((T, D), x2d.dtype),
        grid=(T // tm,),
        in_specs=[
            pl.BlockSpec((tm, D), lambda i: (i, 0)),
            pl.BlockSpec((tm, A_WIDTH), lambda i: (i, 0)),
            pl.BlockSpec((tm, B_WIDTH), lambda i: (i, 0)),
            *gate_specs,
            const((A_WIDTH, D)), const((B_WIDTH, D)), const((D, D)), const((1, D)),
        ],
        out_specs=pl.BlockSpec((tm, D), lambda i: (i, 0)),
        compiler_params=pltpu.CompilerParams(
            dimension_semantics=("parallel",), vmem_limit_bytes=VMEM_LIMIT_BYTES),
        name="out_proj",
    )(x2d, oa, ob, *([z] * (2 * n_gate_blocks)), wa, wb, wo, fg)


def _rope_tables(seq):
    rows = seq // GRID_W
    row = jnp.repeat(jnp.arange(rows), GRID_W).astype(jnp.float32)
    col = jnp.tile(jnp.arange(GRID_W), rows).astype(jnp.float32)
    inv = ROPE_THETA ** (-jnp.arange(0, ROPE_SECTION, 2, dtype=jnp.float32) / ROPE_SECTION)
    half = ROPE_SECTION // 2
    zeros = jnp.zeros((seq, half), jnp.float32)
    cos_parts, sina_parts, sinb_parts = [], [], []
    for pos in (row, col):
        ang = pos[:, None] * inv[None, :]
        c, s = jnp.cos(ang), jnp.sin(ang)
        cos_parts += [c, c]
        sina_parts += [-s, zeros]
        sinb_parts += [zeros, s]
    cat = lambda parts: jnp.concatenate(parts, axis=-1)
    return cat(cos_parts), cat(sina_parts), cat(sinb_parts)


def kernel(x, norm_g, w_in, a_lambda_q1, a_lambda_k1, a_lambda_q2, a_lambda_k2, a_subln_g, b_qnorm_g,
           b_knorm_g, w_proj_a, w_proj_b, w_out, final_g):
    B, S, D = x.shape
    depth = norm_g.shape[0]
    T = B * S
    layout, n_cols = _slab_layout(D)
    assert n_cols == w_in.shape[-1]
    bf16 = jnp.bfloat16

    tm_in = min(1024, S)
    tn_in = 512
    tq_a = min(512, S)
    tq_b = min(512, S)
    tm_out = min(512, T)

    cos_t, sina_t, sinb_t = _rope_tables(S)
    slopes = 2.0 ** (-8.0 * jnp.arange(1, A_HEADS + 1, dtype=jnp.float32) / A_HEADS) * LOG2E

    h = x.reshape(T, D)
    for l in range(depth):
        lam_init = 0.8 - 0.6 * math.exp(-0.3 * l)
        z = _in_proj(h, norm_g[l][None, :], w_in[l].astype(bf16), cos_t, sina_t, sinb_t, b_qnorm_g[l][None, :],
                     b_knorm_g[l][None, :], seq=S, layout=layout, n_cols=n_cols, tm=tm_in, tn=tn_in)
        oa = _attn_a(z, slopes, a_lambda_q1[l][None, :], a_lambda_k1[l][None, :], a_lambda_q2[l][None, :],
                     a_lambda_k2[l][None, :], a_subln_g[l][None, :], batch=B, seq=S, layout=layout,
                     tq=tq_a, lam_init=lam_init)
        ob = _attn_b(z, batch=B, seq=S, layout=layout, tq=tq_b)
        h = _out_proj(h, oa, ob, z, w_proj_a[l].astype(bf16), w_proj_b[l].astype(bf16),
                      w_out[l].astype(bf16), final_g[None, :], layout=layout, gate_block=tn_in,
                      tm=tm_out, final_norm=(l == depth - 1))
    return h.reshape(B, S, D)
```

```python
import functools
import math

import jax
import jax.numpy as jnp
from jax import lax
from jax.experimental import pallas as pl
from jax.experimental.pallas import tpu as pltpu

GRID_W = 64
A_HEADS = 8
A_HEAD_DIM = 64
A_V_DIM = 2 * A_HEAD_DIM
A_WIDTH = A_HEADS * A_V_DIM
B_HEADS = 8
B_KV_HEADS = 2
B_GROUP = B_HEADS // B_KV_HEADS
B_HEAD_DIM = 128
B_WIDTH = B_HEADS * B_HEAD_DIM
ROPE_SECTION = B_HEAD_DIM // 2
ROPE_THETA = 10000.0
NORM_EPS = 1e-6
LOG2E = math.log2(math.e)

LANES = 128
VMEM_LIMIT_BYTES = 56 * 1024 * 1024


def _slab_layout(d_model):
    sizes = (
        ("aq", A_WIDTH), ("ak", A_WIDTH), ("av", A_WIDTH), ("ag", A_WIDTH),
        ("bq", B_WIDTH), ("bk", B_KV_HEADS * B_HEAD_DIM), ("bv", B_KV_HEADS * B_HEAD_DIM),
        ("bg", B_WIDTH),
        ("gm", 2 * d_model),
    )
    off, out = 0, {}
    for name, n in sizes:
        out[name] = (off, n)
        off += n
    return out, off


def _silu(t):
    return t * jax.nn.sigmoid(t)


def _norm_rope(t, gain, cos, sina, sinb, scale):
    ms = jnp.mean(t * t, axis=-1, keepdims=True)
    n = t * lax.rsqrt(ms + NORM_EPS) * gain
    r = n * cos + pltpu.roll(n, LANES - ROPE_SECTION // 2, 1) * sina + pltpu.roll(n, ROPE_SECTION // 2, 1) * sinb
    return r * scale


_SEGMENT_KIND = {"aq": "scale_a", "ak": "plain", "av": "plain", "ag": "silu", "bq": "rope_q", "bk": "rope_k",
                 "bv": "plain", "bg": "silu", "gm": "sigmoid"}


def _tile_plans(layout, tn, n_cols):
    def kind_at(col):
        for name, (off, n) in layout.items():
            if off <= col < off + n:
                return _SEGMENT_KIND[name]
        raise ValueError(col)

    plans = []
    for j in range(n_cols // tn):
        runs = []
        for c in range(0, tn, LANES):
            kind = kind_at(j * tn + c)
            if runs and runs[-1][2] == kind:
                runs[-1] = (runs[-1][0], c + LANES, kind)
            else:
                runs.append((c, c + LANES, kind))
        plans.append(tuple(runs))
    return plans


def _in_proj_kernel(x_ref, g_ref, w_ref, cos_ref, sina_ref, sinb_ref, qg_ref, kg_ref, z_ref, u_ref,
                    *, plans, a_scale, b_scale, row_chunk):
    j = pl.program_id(1)
    tm = x_ref.shape[0]
    n_chunks = tm // row_chunk

    @pl.when(j == 0)
    def _():
        def body(r, c):
            rows = pl.ds(pl.multiple_of(r * row_chunk, row_chunk), row_chunk)
            xr = x_ref[rows, :]
            ms = jnp.mean(xr * xr, axis=-1, keepdims=True)
            u_ref[rows, :] = (xr * lax.rsqrt(ms + NORM_EPS) * g_ref[...]).astype(u_ref.dtype)
            return c
        lax.fori_loop(0, n_chunks, body, 0)

    def rope(t, gain_ref, rows, scale):
        return _norm_rope(t, gain_ref[...], cos_ref[rows, :], sina_ref[rows, :], sinb_ref[rows, :], scale)

    def apply(kind, t, rows):
        if kind == "plain":
            return t
        if kind == "scale_a":
            return t * a_scale
        if kind == "silu":
            return _silu(t)
        if kind == "sigmoid":
            return jax.nn.sigmoid(t)
        gain_ref, scale = (qg_ref, b_scale) if kind == "rope_q" else (kg_ref, 1.0)
        heads = [rope(t[:, c:c + LANES], gain_ref, rows, scale) for c in range(0, t.shape[1], LANES)]
        return jnp.concatenate(heads, axis=-1)

    def project(plan):
        for r in range(n_chunks):
            rows = slice(r * row_chunk, (r + 1) * row_chunk)
            acc = jnp.dot(u_ref[rows, :], w_ref[...], preferred_element_type=jnp.float32)
            for lo, hi, kind in plan:
                z_ref[rows, lo:hi] = apply(kind, acc[:, lo:hi], rows).astype(z_ref.dtype)

    tiles_of_plan = {}
    for jj, plan in enumerate(plans):
        tiles_of_plan.setdefault(plan, []).append(jj)
    for plan, tiles in tiles_of_plan.items():
        cond = functools.reduce(jnp.logical_or, [j == jj for jj in tiles])
        pl.when(cond)(functools.partial(project, plan))


def _in_proj(x2d, g, w_bf, cos_t, sina_t, sinb_t, qg, kg, *, seq, layout, n_cols, tm, tn):
    T, D = x2d.shape
    assert T % tm == 0 and seq % tm == 0 and n_cols % tn == 0
    assert all(off % LANES == 0 and n % LANES == 0 for off, n in layout.values())
    seq_tiles = seq // tm
    kern = functools.partial(
        _in_proj_kernel, plans=_tile_plans(layout, tn, n_cols), a_scale=A_HEAD_DIM ** -0.5 * LOG2E,
        b_scale=B_HEAD_DIM ** -0.5 * LOG2E, row_chunk=min(tm, 256))
    tab_spec = pl.BlockSpec((tm, LANES), lambda i, j: (i % seq_tiles, 0))
    vec_spec = pl.BlockSpec((1, LANES), lambda i, j: (0, 0))
    return pl.pallas_call(
        kern,
        out_shape=jax.ShapeDtypeStruct((T, n_cols), jnp.bfloat16),
        grid=(T // tm, n_cols // tn),
        in_specs=[
            pl.BlockSpec((tm, D), lambda i, j: (i, 0)),
            pl.BlockSpec((1, D), lambda i, j: (0, 0)),
            pl.BlockSpec((D, tn), lambda i, j: (0, j)),
            tab_spec, tab_spec, tab_spec, vec_spec, vec_spec,
        ],
        out_specs=pl.BlockSpec((tm, tn), lambda i, j: (i, j)),
        scratch_shapes=[pltpu.VMEM((tm, D), jnp.bfloat16)],
        compiler_params=pltpu.CompilerParams(
            dimension_semantics=("parallel", "arbitrary"), vmem_limit_bytes=VMEM_LIMIT_BYTES),
        name="in_proj",
    )(x2d, g, w_bf, cos_t, sina_t, sinb_t, qg, kg)


_NT = (((1,), (1,)), ((), ()))
ONES_ROWS = 16


def _fill_vxt(vxt_ref, v_ref):
    d = v_ref.shape[1]
    vxt_ref[:d, :] = v_ref[...].astype(jnp.float32).T.astype(vxt_ref.dtype)
    vxt_ref[d:, :] = jnp.ones((vxt_ref.shape[0] - d, vxt_ref.shape[1]), vxt_ref.dtype)


def _softmax_pv(t, vxt):
    d = vxt.shape[0] - ONES_ROWS
    e = jnp.exp2(t - jnp.max(t, axis=0, keepdims=True)).astype(vxt.dtype)
    ox = jnp.dot(vxt, e, preferred_element_type=jnp.float32)
    return ox[:d, :], ox[d:d + 1, :]


def _pipelined_heads(n, scores, vxts, finish, depth=3):
    pending = {g: scores(g) for g in range(min(depth, n))}
    for g in range(n):
        if g + depth < n:
            pending[g + depth] = scores(g + depth)
        ot, l = _softmax_pv(pending.pop(g), vxts(g))
        finish(g, ot, l)


def _attn_a_kernel(slopes_ref, q_ref, k_ref, v_ref, gate_ref, lq1_ref, lk1_ref, lq2_ref, lk2_ref, sg_ref,
                   o_ref, bias_ref, vxt_ref, *, lam_init):
    hg = pl.program_id(0)
    b = pl.program_id(1)
    qi = pl.program_id(2)
    tq = q_ref.shape[0]
    S = k_ref.shape[0]
    n_heads = bias_ref.shape[0]
    pad = S - tq

    @pl.when(jnp.logical_and(b == 0, qi == 0))
    def _():
        rows = lax.broadcasted_iota(jnp.int32, (bias_ref.shape[1], 1), 0).astype(jnp.float32)
        cols = lax.broadcasted_iota(jnp.int32, (1, tq), 1).astype(jnp.float32)
        dist = jnp.abs(rows - cols - float(pad))
        for g in range(n_heads):
            bias_ref[g] = dist * (-slopes_ref[hg * n_heads + g])

    @pl.when(qi == 0)
    def _():
        for g in range(n_heads):
            _fill_vxt(vxt_ref.at[g], v_ref.at[:, g * A_V_DIM:(g + 1) * A_V_DIM])

    lam = (jnp.exp(jnp.sum(lq1_ref[...] * lk1_ref[...], axis=-1, keepdims=True))
           - jnp.exp(jnp.sum(lq2_ref[...] * lk2_ref[...], axis=-1, keepdims=True)) + lam_init)
    window = pl.ds(pl.multiple_of(pad - qi * tq, tq), S)
    lane = lax.broadcasted_iota(jnp.int32, (1, LANES), 1)

    def scores(g):
        sl = slice(g * LANES, (g + 1) * LANES)
        q = q_ref[:, sl]
        zero = jnp.zeros_like(q)
        qq = jnp.concatenate([jnp.where(lane < A_HEAD_DIM, q, zero), jnp.where(lane >= A_HEAD_DIM, q, zero)], axis=0)
        t = lax.dot_general(k_ref[:, sl], qq, _NT, preferred_element_type=jnp.float32)
        bias = bias_ref[g, window, :]
        return jnp.concatenate([t[:, :tq] + bias, t[:, tq:] + bias], axis=1)

    def finish(g, ot, l):
        sl = slice(g * LANES, (g + 1) * LANES)
        o = (ot[:, :tq] * (1.0 / l[:, :tq]) - ot[:, tq:] * (lam / l[:, tq:])).T
        ms = jnp.mean(o * o, axis=-1, keepdims=True)
        o = o * lax.rsqrt(ms + NORM_EPS) * sg_ref[...] * (1.0 - lam_init)
        o_ref[:, sl] = (o * gate_ref[:, sl].astype(jnp.float32)).astype(o_ref.dtype)

    _pipelined_heads(n_heads, scores, lambda g: vxt_ref[g], finish)


def _attn_a(z, slopes, lq1, lk1, lq2, lk2, sg, *, batch, seq, layout, tq, heads_per_step, lam_init):
    T = z.shape[0]
    nq = seq // tq
    gw = heads_per_step * LANES
    assert A_HEADS % heads_per_step == 0 and all(layout[n][0] % gw == 0 for n in ("aq", "ak", "av", "ag"))
    cb = lambda name: layout[name][0] // gw
    aq, ak, av, ag = cb("aq"), cb("ak"), cb("av"), cb("ag")
    lam_spec = pl.BlockSpec((1, A_HEAD_DIM), lambda h, b, i: (0, 0))
    return pl.pallas_call(
        functools.partial(_attn_a_kernel, lam_init=lam_init),
        out_shape=jax.ShapeDtypeStruct((T, A_WIDTH), jnp.bfloat16),
        grid=(A_HEADS // heads_per_step, batch, nq),
        in_specs=[
            pl.BlockSpec(memory_space=pltpu.SMEM),
            pl.BlockSpec((tq, gw), lambda h, b, i: (b * nq + i, aq + h)),
            pl.BlockSpec((seq, gw), lambda h, b, i: (b, ak + h)),
            pl.BlockSpec((seq, gw), lambda h, b, i: (b, av + h)),
            pl.BlockSpec((tq, gw), lambda h, b, i: (b * nq + i, ag + h)),
            lam_spec, lam_spec, lam_spec, lam_spec,
            pl.BlockSpec((1, LANES), lambda h, b, i: (0, 0)),
        ],
        out_specs=pl.BlockSpec((tq, gw), lambda h, b, i: (b * nq + i, h)),
        scratch_shapes=[pltpu.VMEM((heads_per_step, 2 * seq - tq, tq), jnp.float32),
                        pltpu.VMEM((heads_per_step, A_V_DIM + ONES_ROWS, seq), jnp.bfloat16)],
        compiler_params=pltpu.CompilerParams(
            dimension_semantics=("arbitrary", "arbitrary", "arbitrary"), vmem_limit_bytes=VMEM_LIMIT_BYTES),
        name="attn_a",
    )(slopes, z, z, z, z, lq1, lk1, lq2, lk2, sg)


def _attn_b_kernel(q_ref, k_ref, v_ref, gate_ref, o_ref, vxt_ref):
    @pl.when(pl.program_id(2) == 0)
    def _():
        _fill_vxt(vxt_ref, v_ref)

    def scores(g):
        return lax.dot_general(k_ref[...], q_ref[:, g * B_HEAD_DIM:(g + 1) * B_HEAD_DIM], _NT,
                               preferred_element_type=jnp.float32)

    def finish(g, ot, l):
        sl = slice(g * B_HEAD_DIM, (g + 1) * B_HEAD_DIM)
        o = (ot * (1.0 / l)).T
        o_ref[:, sl] = (o * gate_ref[:, sl].astype(jnp.float32)).astype(o_ref.dtype)

    _pipelined_heads(B_GROUP, scores, lambda g: vxt_ref[...], finish)


def _attn_b(z, *, batch, seq, layout, tq):
    T = z.shape[0]
    nq = seq // tq
    gw = B_GROUP * B_HEAD_DIM
    bq = layout["bq"][0] // gw
    bg = layout["bg"][0] // gw
    bk = layout["bk"][0] // B_HEAD_DIM
    bv = layout["bv"][0] // B_HEAD_DIM
    assert layout["bq"][0] % gw == 0 and layout["bg"][0] % gw == 0
    return pl.pallas_call(
        _attn_b_kernel,
        out_shape=jax.ShapeDtypeStruct((T, B_WIDTH), jnp.bfloat16),
        grid=(batch, B_KV_HEADS, nq),
        in_specs=[
            pl.BlockSpec((tq, gw), lambda b, c, i: (b * nq + i, bq + c)),
            pl.BlockSpec((seq, B_HEAD_DIM), lambda b, c, i: (b, bk + c)),
            pl.BlockSpec((seq, B_HEAD_DIM), lambda b, c, i: (b, bv + c)),
            pl.BlockSpec((tq, gw), lambda b, c, i: (b * nq + i, bg + c)),
        ],
        out_specs=pl.BlockSpec((tq, gw), lambda b, c, i: (b * nq + i, c)),
        scratch_shapes=[pltpu.VMEM((B_HEAD_DIM + ONES_ROWS, seq), jnp.bfloat16)],
        compiler_params=pltpu.CompilerParams(
            dimension_semantics=("parallel", "parallel", "arbitrary"), vmem_limit_bytes=VMEM_LIMIT_BYTES),
        name="attn_b",
    )(z, z, z, z)


def _out_proj_kernel(*refs, n_gate_blocks, final_norm):
    x_ref, oa_ref, ob_ref = refs[:3]
    ga_refs = refs[3:3 + n_gate_blocks]
    gb_refs = refs[3 + n_gate_blocks:3 + 2 * n_gate_blocks]
    wa_ref, wb_ref, wo_ref, fg_ref, o_ref = refs[3 + 2 * n_gate_blocks:]
    ya = jnp.dot(oa_ref[...], wa_ref[...], preferred_element_type=jnp.float32)
    yb = jnp.dot(ob_ref[...], wb_ref[...], preferred_element_type=jnp.float32)
    ga = jnp.concatenate([r[...] for r in ga_refs], axis=-1).astype(jnp.float32)
    gb = jnp.concatenate([r[...] for r in gb_refs], axis=-1).astype(jnp.float32)
    m = ga * ya + gb * yb
    hres = x_ref[...] + jnp.dot(m.astype(wo_ref.dtype), wo_ref[...], preferred_element_type=jnp.float32)
    if final_norm:
        ms = jnp.mean(hres * hres, axis=-1, keepdims=True)
        hres = hres * lax.rsqrt(ms + NORM_EPS) * fg_ref[...]
    o_ref[...] = hres.astype(o_ref.dtype)


def _out_proj(x2d, oa, ob, z, wa, wb, wo, fg, *, layout, tm, final_norm):
    T, D = x2d.shape
    gm_off, gm_n = layout["gm"]
    gate_block = math.gcd(gm_off, D)
    assert gm_n == 2 * D and gate_block % LANES == 0
    n_gate_blocks = D // gate_block
    const = lambda shape: pl.BlockSpec(shape, lambda i: (0, 0), pipeline_mode=pl.Buffered(1))
    gate_spec = lambda c: pl.BlockSpec((tm, gate_block), lambda i: (i, c))
    first = gm_off // gate_block
    gate_specs = [gate_spec(first + c) for c in range(2 * n_gate_blocks)]
    return pl.pallas_call(
        functools.partial(_out_proj_kernel, n_gate_blocks=n_gate_blocks, final_norm=final_norm),
        out_shape=jax.ShapeDtypeStruct((T, D), x2d.dtype),
        grid=(T // tm,),
        in_specs=[
            pl.BlockSpec((tm, D), lambda i: (i, 0)),
            pl.BlockSpec((tm, A_WIDTH), lambda i: (i, 0)),
            pl.BlockSpec((tm, B_WIDTH), lambda i: (i, 0)),
            *gate_specs,
            const((A_WIDTH, D)), const((B_WIDTH, D)), const((D, D)), const((1, D)),
        ],
        out_specs=pl.BlockSpec((tm, D), lambda i: (i, 0)),
        compiler_params=pltpu.CompilerParams(
            dimension_semantics=("parallel",), vmem_limit_bytes=VMEM_LIMIT_BYTES),
        name="out_proj",
    )(x2d, oa, ob, *([z] * (2 * n_gate_blocks)), wa, wb, wo, fg)


def _rope_tables(seq):
    rows = seq // GRID_W
    row = jnp.repeat(jnp.arange(rows), GRID_W).astype(jnp.float32)
    col = jnp.tile(jnp.arange(GRID_W), rows).astype(jnp.float32)
    inv = ROPE_THETA ** (-jnp.arange(0, ROPE_SECTION, 2, dtype=jnp.float32) / ROPE_SECTION)
    half = ROPE_SECTION // 2
    zeros = jnp.zeros((seq, half), jnp.float32)
    cos_parts, sina_parts, sinb_parts = [], [], []
    for pos in (row, col):
        ang = pos[:, None] * inv[None, :]
        c, s = jnp.cos(ang), jnp.sin(ang)
        cos_parts += [c, c]
        sina_parts += [-s, zeros]
        sinb_parts += [zeros, s]
    cat = lambda parts: jnp.concatenate(parts, axis=-1)
    return cat(cos_parts), cat(sina_parts), cat(sinb_parts)


def kernel(x, norm_g, w_in, a_lambda_q1, a_lambda_k1, a_lambda_q2, a_lambda_k2, a_subln_g, b_qnorm_g,
           b_knorm_g, w_proj_a, w_proj_b, w_out, final_g):
    B, S, D = x.shape
    depth = norm_g.shape[0]
    T = B * S
    layout, n_cols = _slab_layout(D)
    assert n_cols == w_in.shape[-1]
    bf16 = jnp.bfloat16

    tm_in = min(1024, S)
    tn_in = 1536
    tq_a = min(256, S)
    tq_b = min(512, S)
    tm_out = min(512, T)

    cos_t, sina_t, sinb_t = _rope_tables(S)
    slopes = 2.0 ** (-8.0 * jnp.arange(1, A_HEADS + 1, dtype=jnp.float32) / A_HEADS) * LOG2E

    h = x.reshape(T, D)
    for l in range(depth):
        lam_init = 0.8 - 0.6 * math.exp(-0.3 * l)
        z = _in_proj(h, norm_g[l][None, :], w_in[l].astype(bf16), cos_t, sina_t, sinb_t, b_qnorm_g[l][None, :],
                     b_knorm_g[l][None, :], seq=S, layout=layout, n_cols=n_cols, tm=tm_in, tn=tn_in)
        oa = _attn_a(z, slopes, a_lambda_q1[l][None, :], a_lambda_k1[l][None, :], a_lambda_q2[l][None, :],
                     a_lambda_k2[l][None, :], a_subln_g[l][None, :], batch=B, seq=S, layout=layout,
                     tq=tq_a, heads_per_step=4, lam_init=lam_init)
        ob = _attn_b(z, batch=B, seq=S, layout=layout, tq=tq_b)
        h = _out_proj(h, oa, ob, z, w_proj_a[l].astype(bf16), w_proj_b[l].astype(bf16),
                      w_out[l].astype(bf16), final_g[None, :], layout=layout,
                      tm=tm_out, final_norm=(l == depth - 1))
    return h.reshape(B, S, D)
```
